```python
import math
import jax
import jax.numpy as jnp
from jax import lax
import numpy as np

D_MODEL = 2048
BATCH = 1
SEQ = 8192
DEPTH = 2

N_MIXERS = 2
MEM_LEN = 256
HEAD_DIM = 128
N_ATTN_HEADS = 12
N_MEM_HEADS = 4
ATTN_WIDTH = N_ATTN_HEADS * HEAD_DIM
MEM_WIDTH = N_MEM_HEADS * HEAD_DIM
MOBA_BLOCK = 256
MOBA_TOPK = 3
MOBA_Q_CHUNK = 32
ROPE_THETA = 500000.0
ROT_DIM = HEAD_DIM // 4
SSD_D_INNER = 1536
SSD_HEAD_DIM = 64
SSD_N_HEADS = SSD_D_INNER // SSD_HEAD_DIM
SSD_N_GROUPS = 4
SSD_HEADS_PER_GROUP = SSD_N_HEADS // SSD_N_GROUPS
SSD_D_STATE = 128
SSD_CONV = 4
SSD_CHUNK = 256
SSD_CONV_DIM = SSD_D_INNER + 2 * SSD_N_GROUPS * SSD_D_STATE
MIX_WIDTH = ATTN_WIDTH + MEM_WIDTH
MOBA_IN_WIDTH = 3 * ATTN_WIDTH + MEM_WIDTH
SSD_IN_WIDTH = SSD_D_INNER + SSD_CONV_DIM + SSD_N_HEADS + MEM_WIDTH
D_FF = 7168
N_EXPERTS = 8
TOP_K = 2
EXPERT_BLOCK = 256
LN_EPS = 1e-5
RMS_EPS = 1e-5
DEEPNORM_ALPHA = (2 * DEPTH) ** 0.25
DEEPNORM_BETA = (8 * DEPTH) ** -0.25

kernel_name = "hybrid_moba_ssd_memxattn_deepnorm_moe"


def _pad_to(t, multiple, axis=1):
    n = t.shape[axis]
    n_pad = -(-n // multiple) * multiple
    if n_pad == n:
        return t
    widths = [(0, 0)] * t.ndim
    widths[axis] = (0, n_pad - n)
    return jnp.pad(t, widths)


def layer_norm(x, g, b):
    xf = x.astype(jnp.float32)
    mu = jnp.mean(xf, axis=-1, keepdims=True)
    xc = xf - mu
    var = jnp.mean(xc * xc, axis=-1, keepdims=True)
    return (xc * lax.rsqrt(var + LN_EPS) * g.astype(jnp.float32) + b.astype(jnp.float32)).astype(x.dtype)


def swiglu(x, w_gate, w_up, w_down):
    return (jax.nn.silu(x @ w_gate) * (x @ w_up)) @ w_down


def rotary_tables(positions, dtype):
    inv_freq = ROPE_THETA ** (-jnp.arange(0, ROT_DIM, 2, dtype=jnp.float32) / ROT_DIM)
    ang = positions.astype(jnp.float32)[..., None] * inv_freq
    return jnp.cos(ang)[:, :, None, :].astype(dtype), jnp.sin(ang)[:, :, None, :].astype(dtype)


def apply_partial_rotary(t, cos, sin):
    half = ROT_DIM // 2
    t1 = t[..., :half]
    t2 = t[..., half:ROT_DIM]
    return jnp.concatenate([t1 * cos - t2 * sin, t2 * cos + t1 * sin, t[..., ROT_DIM:]], axis=-1)


def memory_attention(mq, mem_k, mem_v):
    s = jnp.einsum("bthd,bmhd->bhtm", mq, mem_k).astype(jnp.float32) * (HEAD_DIM ** -0.5)
    p = jax.nn.softmax(s, axis=-1).astype(mem_v.dtype)
    return jnp.einsum("bhtm,bmhd->bthd", p, mem_v)


def moba_attention(q, k, v):
    bsz, seq, n_heads, hd = q.shape
    qh = _pad_to(q, MOBA_BLOCK).transpose(0, 2, 1, 3)
    t_pad = qh.shape[2]
    nb = t_pad // MOBA_BLOCK
    kb = _pad_to(k, MOBA_BLOCK).transpose(0, 2, 1, 3).reshape(bsz, n_heads, nb, MOBA_BLOCK, hd)
    vb = _pad_to(v, MOBA_BLOCK).transpose(0, 2, 1, 3).reshape(bsz, n_heads, nb, MOBA_BLOCK, hd)
    k_mean = jnp.mean(kb.astype(jnp.float32), axis=3)
    gate = jnp.einsum("bhtd,bhnd->bhtn", qh.astype(jnp.float32), k_mean)
    q_block = jnp.arange(t_pad) // MOBA_BLOCK
    fully_past = jnp.arange(nb)[None, :] < q_block[:, None]
    gate = jnp.where(fully_past, gate, -jnp.inf)
    n_sel = min(MOBA_TOPK, nb)
    _, sel = lax.top_k(gate, n_sel)
    sel_valid = jnp.arange(n_sel)[None, :] < q_block[:, None]
    b_idx = jnp.arange(bsz)[:, None, None, None]
    h_idx = jnp.arange(n_heads)[None, :, None, None]
    scale = hd ** -0.5

    def chunk_fn(c):
        t0 = c * MOBA_Q_CHUNK
        qc = lax.dynamic_slice_in_dim(qh, t0, MOBA_Q_CHUNK, axis=2)
        selc = lax.dynamic_slice_in_dim(sel, t0, MOBA_Q_CHUNK, axis=2)
        validc = lax.dynamic_slice_in_dim(sel_valid, t0, MOBA_Q_CHUNK, axis=0)
        own = t0 // MOBA_BLOCK
        k_own = lax.dynamic_index_in_dim(kb, own, axis=2, keepdims=False)
        v_own = lax.dynamic_index_in_dim(vb, own, axis=2, keepdims=False)
        k_sel = kb[b_idx, h_idx, selc]
        v_sel = vb[b_idx, h_idx, selc]
        s_sel = jnp.einsum("bhqd,bhqnsd->bhqns", qc, k_sel).astype(jnp.float32) * scale
        s_sel = jnp.where(validc[None, None, :, :, None], s_sel, -jnp.inf)
        s_own = jnp.einsum("bhqd,bhsd->bhqs", qc, k_own).astype(jnp.float32) * scale
        causal = (t0 + jnp.arange(MOBA_Q_CHUNK))[:, None] >= own * MOBA_BLOCK + jnp.arange(MOBA_BLOCK)[None, :]
        s_own = jnp.where(causal, s_own, -jnp.inf)
        s = jnp.concatenate([s_sel.reshape(bsz, n_heads, MOBA_Q_CHUNK, n_sel * MOBA_BLOCK), s_own], axis=-1)
        p = jax.nn.softmax(s, axis=-1).astype(vb.dtype)
        p_sel = p[..., :n_sel * MOBA_BLOCK].reshape(bsz, n_heads, MOBA_Q_CHUNK, n_sel, MOBA_BLOCK)
        p_own = p[..., n_sel * MOBA_BLOCK:]
        return (jnp.einsum("bhqns,bhqnsd->bhqd", p_sel, v_sel)
                + jnp.einsum("bhqs,bhsd->bhqd", p_own, v_own))

    out = lax.map(chunk_fn, jnp.arange(t_pad // MOBA_Q_CHUNK))
    out = out.transpose(1, 0, 3, 2, 4).reshape(bsz, t_pad, n_heads, hd)
    return out[:, :seq]


def causal_depthwise_conv(u, w, b):
    n_ch = u.shape[-1]
    out = lax.conv_general_dilated(
        u, w[:, None, :].astype(u.dtype), window_strides=(1,), padding=[(SSD_CONV - 1, 0)],
        dimension_numbers=("NWC", "WIO", "NWC"), feature_group_count=n_ch)
    return out + b


def ssd_chunked(xs, dt, a, bm, cm):
    bsz, seq, n_heads, hp = xs.shape
    f32 = jnp.float32
    xs = _pad_to(xs.astype(f32), SSD_CHUNK)
    dt = _pad_to(dt.astype(f32), SSD_CHUNK)
    bm = _pad_to(bm.astype(f32), SSD_CHUNK)
    cm = _pad_to(cm.astype(f32), SSD_CHUNK)
    t_pad = xs.shape[1]
    nc = t_pad // SSD_CHUNK
    g, e, n = SSD_N_GROUPS, SSD_HEADS_PER_GROUP, SSD_D_STATE
    xdt = (xs * dt[..., None]).reshape(bsz, nc, SSD_CHUNK, g, e, hp)
    adt = (dt * a).reshape(bsz, nc, SSD_CHUNK, g, e).transpose(0, 3, 4, 1, 2)
    bc = bm.reshape(bsz, nc, SSD_CHUNK, g, n)
    cc = cm.reshape(bsz, nc, SSD_CHUNK, g, n)
    a_cs = jnp.cumsum(adt, axis=-1)
    tril = jnp.arange(SSD_CHUNK)[:, None] >= jnp.arange(SSD_CHUNK)[None, :]
    decay = jnp.exp(jnp.where(tril, a_cs[..., :, None] - a_cs[..., None, :], -jnp.inf))
    cb = jnp.einsum("bclgn,bcsgn->bgcls", cc, bc)
    y_diag = jnp.einsum("bgecls,bcsgep->bclgep", cb[:, :, None] * decay, xdt)
    decay_states = jnp.exp(a_cs[..., -1:] - a_cs)
    states = jnp.einsum("bclgn,bgecl,bclgep->bcgepn", bc, decay_states, xdt)
    chunk_decay = jnp.exp(a_cs[..., -1])

    def step(carry, inp):
        st, dec = inp
        return carry * dec[..., None, None] + st, carry

    init = jnp.zeros((bsz, g, e, hp, n), f32)
    _, prev_states = lax.scan(step, init, (jnp.moveaxis(states, 1, 0), jnp.moveaxis(chunk_decay, -1, 0)))
    prev_states = jnp.moveaxis(prev_states, 0, 1)
    y_off = jnp.einsum("bclgn,bcgepn,bgecl->bclgep", cc, prev_states, jnp.exp(a_cs))
    y = (y_diag + y_off).reshape(bsz, t_pad, n_heads, hp)
    return y[:, :seq]


def moba_mixer(x, positions, mem_k, mem_v, w_in, w_o):
    bsz, seq, _ = x.shape
    h = x @ w_in
    q, k, v, mq = jnp.split(h, [ATTN_WIDTH, 2 * ATTN_WIDTH, 3 * ATTN_WIDTH], axis=-1)
    q = q.reshape(bsz, seq, N_ATTN_HEADS, HEAD_DIM)
    k = k.reshape(bsz, seq, N_ATTN_HEADS, HEAD_DIM)
    v = v.reshape(bsz, seq, N_ATTN_HEADS, HEAD_DIM)
    cos, sin = rotary_tables(positions, x.dtype)
    q = apply_partial_rotary(q, cos, sin)
    k = apply_partial_rotary(k, cos, sin)
    a = moba_attention(q, k, v)
    m = memory_attention(mq.reshape(bsz, seq, N_MEM_HEADS, HEAD_DIM), mem_k, mem_v)
    heads = jnp.concatenate([a.reshape(bsz, seq, ATTN_WIDTH), m.reshape(bsz, seq, MEM_WIDTH)], axis=-1)
    return heads @ w_o


def ssd_mixer(x, mem_k, mem_v, w_in, conv_w, conv_b, dt_bias, a_log, d_skip, norm_g, w_o):
    bsz, seq, _ = x.shape
    h = x @ w_in
    z, xbc, dt, mq = jnp.split(
        h, [SSD_D_INNER, SSD_D_INNER + SSD_CONV_DIM, SSD_D_INNER + SSD_CONV_DIM + SSD_N_HEADS], axis=-1)
    xbc = jax.nn.silu(causal_depthwise_conv(xbc, conv_w, conv_b))
    xs, bm, cm = jnp.split(xbc, [SSD_D_INNER, SSD_D_INNER + SSD_N_GROUPS * SSD_D_STATE], axis=-1)
    xs = xs.reshape(bsz, seq, SSD_N_HEADS, SSD_HEAD_DIM)
    bm = bm.reshape(bsz, seq, SSD_N_GROUPS, SSD_D_STATE)
    cm = cm.reshape(bsz, seq, SSD_N_GROUPS, SSD_D_STATE)
    dt = jax.nn.softplus((dt + dt_bias).astype(jnp.float32))
    a = -jnp.exp(a_log.astype(jnp.float32))
    y = ssd_chunked(xs, dt, a, bm, cm)
    y = y + d_skip.astype(jnp.float32)[:, None] * xs.astype(jnp.float32)
    y = y.reshape(bsz, seq, SSD_D_INNER)
    yg = y * jax.nn.silu(z.astype(jnp.float32))
    yg = yg * lax.rsqrt(jnp.mean(yg * yg, axis=-1, keepdims=True) + RMS_EPS) * norm_g.astype(jnp.float32)
    m = memory_attention(mq.reshape(bsz, seq, N_MEM_HEADS, HEAD_DIM), mem_k, mem_v)
    heads = jnp.concatenate([yg.astype(x.dtype), m.reshape(bsz, seq, MEM_WIDTH)], axis=-1)
    return heads @ w_o


def moe_swiglu(x, w_router, w_gate_e, w_up_e, w_down_e):
    bsz, seq, d = x.shape
    xt = x.reshape(bsz * seq, d)
    n_tok = xt.shape[0]
    logits = (xt @ w_router).astype(jnp.float32)
    top_val, top_idx = lax.top_k(logits, TOP_K)
    gates = jax.nn.softmax(top_val, axis=-1).astype(x.dtype)
    n_assign = n_tok * TOP_K
    flat_e = top_idx.reshape(n_assign)
    flat_tok = jnp.repeat(jnp.arange(n_tok, dtype=jnp.int32), TOP_K)
    flat_w = gates.reshape(n_assign)
    order = jnp.argsort(flat_e)
    sorted_e = flat_e[order]
    counts = jnp.bincount(flat_e, length=N_EXPERTS)
    starts = jnp.cumsum(counts) - counts
    padded = (counts + EXPERT_BLOCK - 1) // EXPERT_BLOCK * EXPERT_BLOCK
    pad_ends = jnp.cumsum(padded)
    pad_starts = pad_ends - padded
    dest = pad_starts[sorted_e] + (jnp.arange(n_assign) - starts[sorted_e])
    n_blocks = -(-n_assign // EXPERT_BLOCK) + N_EXPERTS
    n_rows = n_blocks * EXPERT_BLOCK
    row_tok = jnp.zeros((n_rows,), jnp.int32).at[dest].set(flat_tok[order])
    row_w = jnp.zeros((n_rows,), x.dtype).at[dest].set(flat_w[order])
    block_e = jnp.minimum(
        jnp.searchsorted(pad_ends, jnp.arange(n_blocks) * EXPERT_BLOCK, side="right"), N_EXPERTS - 1)

    def block_fn(args):
        tok, e = args
        return swiglu(xt[tok], w_gate_e[e], w_up_e[e], w_down_e[e])

    y_rows = lax.map(block_fn, (row_tok.reshape(n_blocks, EXPERT_BLOCK), block_e))
    out = jnp.zeros_like(xt).at[row_tok].add(row_w[:, None] * y_rows.reshape(n_rows, d))
    return out.reshape(bsz, seq, d)


def setup_inputs(seed: int = 0) -> dict:
    key = jax.random.key(seed)
    ks = iter(jax.random.split(key, 48))
    f32 = jnp.float32

    def nrm(shape, scale):
        return jax.random.normal(next(ks), shape, f32) * scale

    def gain(n):
        return 1.0 + nrm((n,), 0.02)

    D = D_MODEL
    x = nrm((BATCH, SEQ, D), 1.0)
    mem = nrm((BATCH, MEM_LEN, D), 1.0)
    positions = jnp.broadcast_to(jnp.arange(SEQ, dtype=jnp.int32), (BATCH, SEQ))
    w_mem_kv = nrm((D, 2 * MEM_WIDTH), D ** -0.5)
    l0_w_in = nrm((D, MOBA_IN_WIDTH), D ** -0.5)
    l0_w_o = nrm((MIX_WIDTH, D), MIX_WIDTH ** -0.5 * DEEPNORM_BETA)
    l0_ln1_g = gain(D)
    l0_ln1_b = nrm((D,), 0.02)
    l0_ffn_w_gate = nrm((D, D_FF), D ** -0.5)
    l0_ffn_w_up = nrm((D, D_FF), D ** -0.5)
    l0_ffn_w_down = nrm((D_FF, D), D_FF ** -0.5 * DEEPNORM_BETA)
    l0_ln2_g = gain(D)
    l0_ln2_b = nrm((D,), 0.02)
    l1_w_in = nrm((D, SSD_IN_WIDTH), D ** -0.5)
    l1_conv_w = nrm((SSD_CONV, SSD_CONV_DIM), SSD_CONV ** -0.5)
    l1_conv_b = nrm((SSD_CONV_DIM,), 0.02)
    dt0 = jnp.exp(jax.random.uniform(next(ks), (SSD_N_HEADS,), f32, math.log(1e-3), math.log(1e-1)))
    l1_dt_bias = dt0 + jnp.log(-jnp.expm1(-dt0))
    l1_a_log = jnp.log(jax.random.uniform(next(ks), (SSD_N_HEADS,), f32, 1.0, 16.0))
    l1_d_skip = 1.0 + nrm((SSD_N_HEADS,), 0.1)
    l1_ssd_norm_g = gain(SSD_D_INNER)
    l1_w_o = nrm((MIX_WIDTH, D), MIX_WIDTH ** -0.5 * DEEPNORM_BETA)
    l1_ln1_g = gain(D)
    l1_ln1_b = nrm((D,), 0.02)
    l1_router = nrm((D, N_EXPERTS), D ** -0.5)
    l1_exp_w_gate = nrm((N_EXPERTS, D, D_FF), D ** -0.5)
    l1_exp_w_up = nrm((N_EXPERTS, D, D_FF), D ** -0.5)
    l1_exp_w_down = nrm((N_EXPERTS, D_FF, D), D_FF ** -0.5 * DEEPNORM_BETA)
    l1_ln2_g = gain(D)
    l1_ln2_b = nrm((D,), 0.02)
    return {
        "x": x, "mem": mem, "positions": positions, "w_mem_kv": w_mem_kv,
        "l0_w_in": l0_w_in, "l0_w_o": l0_w_o, "l0_ln1_g": l0_ln1_g, "l0_ln1_b": l0_ln1_b,
        "l0_ffn_w_gate": l0_ffn_w_gate, "l0_ffn_w_up": l0_ffn_w_up, "l0_ffn_w_down": l0_ffn_w_down,
        "l0_ln2_g": l0_ln2_g, "l0_ln2_b": l0_ln2_b,
        "l1_w_in": l1_w_in, "l1_conv_w": l1_conv_w, "l1_conv_b": l1_conv_b,
        "l1_dt_bias": l1_dt_bias, "l1_a_log": l1_a_log, "l1_d_skip": l1_d_skip,
        "l1_ssd_norm_g": l1_ssd_norm_g, "l1_w_o": l1_w_o, "l1_ln1_g": l1_ln1_g, "l1_ln1_b": l1_ln1_b,
        "l1_router": l1_router, "l1_exp_w_gate": l1_exp_w_gate, "l1_exp_w_up": l1_exp_w_up,
        "l1_exp_w_down": l1_exp_w_down, "l1_ln2_g": l1_ln2_g, "l1_ln2_b": l1_ln2_b,
    }


def reference(x, mem, positions, w_mem_kv,
              l0_w_in, l0_w_o, l0_ln1_g, l0_ln1_b, l0_ffn_w_gate, l0_ffn_w_up, l0_ffn_w_down,
              l0_ln2_g, l0_ln2_b,
              l1_w_in, l1_conv_w, l1_conv_b, l1_dt_bias, l1_a_log, l1_d_skip, l1_ssd_norm_g,
              l1_w_o, l1_ln1_g, l1_ln1_b, l1_router, l1_exp_w_gate, l1_exp_w_up, l1_exp_w_down,
              l1_ln2_g, l1_ln2_b):
    bsz = mem.shape[0]
    mem_kv = (mem @ w_mem_kv).reshape(bsz, MEM_LEN, 2, N_MEM_HEADS, HEAD_DIM)
    mem_k = mem_kv[:, :, 0]
    mem_v = mem_kv[:, :, 1]
    layer_params = [
        {"w_in": l0_w_in, "w_o": l0_w_o, "ln1_g": l0_ln1_g, "ln1_b": l0_ln1_b,
         "ffn": (l0_ffn_w_gate, l0_ffn_w_up, l0_ffn_w_down), "ln2_g": l0_ln2_g, "ln2_b": l0_ln2_b},
        {"w_in": l1_w_in, "w_o": l1_w_o, "ln1_g": l1_ln1_g, "ln1_b": l1_ln1_b,
         "ssd": (l1_conv_w, l1_conv_b, l1_dt_bias, l1_a_log, l1_d_skip, l1_ssd_norm_g),
         "moe": (l1_router, l1_exp_w_gate, l1_exp_w_up, l1_exp_w_down),
         "ln2_g": l1_ln2_g, "ln2_b": l1_ln2_b},
    ]
    for i in range(DEPTH):
        p = layer_params[i]
        if i % N_MIXERS == 0:
            y = moba_mixer(x, positions, mem_k, mem_v, p["w_in"], p["w_o"])
        else:
            y = ssd_mixer(x, mem_k, mem_v, p["w_in"], *p["ssd"], p["w_o"])
        x = layer_norm(DEEPNORM_ALPHA * x + y, p["ln1_g"], p["ln1_b"])
        if i % 2 == 0:
            f = swiglu(x, *p["ffn"])
        else:
            f = moe_swiglu(x, *p["moe"])
        x = layer_norm(DEEPNORM_ALPHA * x + f, p["ln2_g"], p["ln2_b"])
    return x
```

```python
import functools

import jax
import jax.numpy as jnp
from jax import lax
from jax.experimental import pallas as pl
from jax.experimental.pallas import tpu as pltpu

F32 = jnp.float32
BF16 = jnp.bfloat16

HEAD_DIM = 128
N_ATTN_HEADS = 12
N_MEM_HEADS = 4
ATTN_WIDTH = N_ATTN_HEADS * HEAD_DIM
MEM_WIDTH = N_MEM_HEADS * HEAD_DIM
MOBA_BLOCK = 256
MOBA_TOPK = 3
ROPE_THETA = 500000.0
ROT_DIM = HEAD_DIM // 4
SSD_D_INNER = 1536
SSD_HEAD_DIM = 64
SSD_N_HEADS = SSD_D_INNER // SSD_HEAD_DIM
SSD_N_GROUPS = 4
SSD_D_STATE = 128
SSD_CONV = 4
SSD_CHUNK = 256
SSD_BC_WIDTH = SSD_N_GROUPS * SSD_D_STATE
SSD_CONV_DIM = SSD_D_INNER + 2 * SSD_BC_WIDTH
N_EXPERTS = 8
TOP_K = 2
LN_EPS = 1e-5
RMS_EPS = 1e-5
DEPTH = 2
DEEPNORM_ALPHA = (2 * DEPTH) ** 0.25

LANES = 128
MXU_ROWS = 256
VMEM_LIMIT_BYTES = 56 * 1024 * 1024

MASK_VALUE = -1e30

PROJ_TM = 1024
PROJ_TN = 512
OPROJ_TM = 256
MEM_TQ = 512
FFN_TM = 1024
FFN_TF = 256
FFN_SUB = 256
MOE_SLOT = 1024
MOE_SUB = 256
MOE_TF = 256
LN_TM = 512


def _params(n_grid_dims):
    return pltpu.CompilerParams(
        dimension_semantics=("arbitrary",) * n_grid_dims,
        vmem_limit_bytes=VMEM_LIMIT_BYTES)


def _silu(v):
    return v * (1.0 / (1.0 + jnp.exp(-v)))


def _split3(v):
    a = v.astype(BF16)
    r = v - a.astype(F32)
    b = r.astype(BF16)
    c = (r - b.astype(F32)).astype(BF16)
    return a, b, c


def _dot(a, b):
    return jnp.dot(a, b, preferred_element_type=F32)


def _layer_norm(r, g, b):
    mu = jnp.mean(r, axis=-1, keepdims=True)
    xc = r - mu
    var = jnp.mean(xc * xc, axis=-1, keepdims=True)
    return xc * lax.rsqrt(var + LN_EPS) * g + b


def _proj_kernel(x_ref, w_ref, o_ref, wb_ref):
    @pl.when(pl.program_id(1) == 0)
    def _():
        wb_ref[...] = w_ref[...].astype(BF16)

    o_ref[...] = _dot(x_ref[...], wb_ref[...]).astype(o_ref.dtype)


def _proj(x, w, n_blocks, col_block0, tn, out_dtype):
    m, k = x.shape
    tm = min(PROJ_TM, m)
    return pl.pallas_call(
        _proj_kernel,
        grid=(n_blocks, m // tm),
        in_specs=[pl.BlockSpec((tm, k), lambda j, i: (i, 0)),
                  pl.BlockSpec((k, tn), lambda j, i: (0, j + col_block0))],
        out_specs=pl.BlockSpec((tm, tn), lambda j, i: (i, j)),
        out_shape=jax.ShapeDtypeStruct((m, n_blocks * tn), out_dtype),
        scratch_shapes=[pltpu.VMEM((k, tn), BF16)],
        compiler_params=_params(2),
    )(x, w)


def _proj_rot_kernel(x_ref, w_ref, c_ref, s1_ref, s2_ref, o_ref, wb_ref, *,
                     n_rot_blocks, n_q_blocks, q_scale):
    j = pl.program_id(0)

    @pl.when(pl.program_id(1) == 0)
    def _():
        wb_ref[...] = w_ref[...].astype(BF16)

    acc = _dot(x_ref[...], wb_ref[...])

    @pl.when(j < n_rot_blocks)
    def _():
        sc = jnp.where(j < n_q_blocks, q_scale, 1.0).astype(F32)
        c = c_ref[...] * sc
        s1 = s1_ref[...] * sc
        s2 = s2_ref[...] * sc
        for hd in range(acc.shape[1] // HEAD_DIM):
            t = acc[:, hd * HEAD_DIM:(hd + 1) * HEAD_DIM]
            half = ROT_DIM // 2
            r = (t * c + pltpu.roll(t, half, 1) * s1
                 + pltpu.roll(t, HEAD_DIM - half, 1) * s2)
            o_ref[:, hd * HEAD_DIM:(hd + 1) * HEAD_DIM] = r.astype(o_ref.dtype)

    @pl.when(j >= n_rot_blocks)
    def _():
        o_ref[...] = acc.astype(o_ref.dtype)


def _rotary_lane_tables(positions):
    half = ROT_DIM // 2
    inv_freq = ROPE_THETA ** (-jnp.arange(0, ROT_DIM, 2, dtype=F32) / ROT_DIM)
    ang = positions.astype(F32)[:, None] * inv_freq
    cos, sin = jnp.cos(ang), jnp.sin(ang)
    t = positions.shape[0]
    ones = jnp.ones((t, HEAD_DIM - ROT_DIM), F32)
    zeros_tail = jnp.zeros((t, HEAD_DIM - ROT_DIM), F32)
    zeros_half = jnp.zeros((t, half), F32)
    c = jnp.concatenate([cos, cos, ones], axis=1)
    s1 = jnp.concatenate([zeros_half, sin, zeros_tail], axis=1)
    s2 = jnp.concatenate([-sin, zeros_half, zeros_tail], axis=1)
    return c, s1, s2


def _proj_rot(x, w, tables, n_blocks, tn, n_rot_blocks, n_q_blocks, q_scale):
    m, k = x.shape
    tm = min(PROJ_TM, m)
    tab_spec = pl.BlockSpec((tm, HEAD_DIM), lambda j, i: (i, 0))
    return pl.pallas_call(
        functools.partial(_proj_rot_kernel, n_rot_blocks=n_rot_blocks,
                          n_q_blocks=n_q_blocks, q_scale=q_scale),
        grid=(n_blocks, m // tm),
        in_specs=[pl.BlockSpec((tm, k), lambda j, i: (i, 0)),
                  pl.BlockSpec((k, tn), lambda j, i: (0, j)),
                  tab_spec, tab_spec, tab_spec],
        out_specs=pl.BlockSpec((tm, tn), lambda j, i: (i, j)),
        out_shape=jax.ShapeDtypeStruct((m, n_blocks * tn), BF16),
        scratch_shapes=[pltpu.VMEM((k, tn), BF16)],
        compiler_params=_params(2),
    )(x, w, *tables)


def _moba_kernel(q_ref, k_ref, v_ref, o_ref, kaug_ref, vt_ref, kmh_ref, kml_ref, qa_ref, *,
                 n_blocks, topk):
    i = pl.program_id(1)
    blk, hd = q_ref.shape

    @pl.when(i == 0)
    def _prepare_head():
        t = k_ref.shape[0]
        row_blk = lax.broadcasted_iota(jnp.int32, (n_blocks, t), 0)
        col_blk = lax.broadcasted_iota(jnp.int32, (n_blocks, t), 1) // blk
        averager = jnp.where(row_blk == col_blk, 1.0 / blk, 0.0).astype(BF16)
        kmean = _dot(averager, k_ref[...])
        kmean_hi = kmean.astype(BF16)
        kmh_ref[...] = kmean_hi
        kml_ref[...] = (kmean - kmean_hi.astype(F32)).astype(BF16)
        lane = lax.broadcasted_iota(jnp.int32, (blk, hd), 1)

        def body(b, carry):
            r0 = pl.multiple_of(b * blk, blk)
            kaug_ref[b, :, 0:hd] = k_ref[pl.ds(r0, blk), :]
            kaug_ref[b, :, hd:2 * hd] = jnp.where(lane == b, 1.0, 0.0).astype(BF16)
            vt_ref[b] = v_ref[pl.ds(r0, blk), :].astype(F32).T.astype(BF16)
            return carry

        lax.fori_loop(0, n_blocks, body, 0)

    q_t = q_ref[...].astype(F32).T.astype(BF16)
    gate = _dot(kmh_ref[...], q_t) + _dot(kml_ref[...], q_t)
    jidx = lax.broadcasted_iota(jnp.int32, gate.shape, 0)
    rank = jnp.zeros(gate.shape, jnp.int32)
    for jp in range(n_blocks):
        row = gate[jp:jp + 1, :]
        beats = jnp.where(row > gate, 1, jnp.where(row == gate, jnp.where(jidx > jp, 1, 0), 0))
        rank = rank + beats * (jp < i).astype(jnp.int32)
    bias = jnp.where(
        jidx == i, 0.0,
        jnp.where(jidx < i, jnp.where(rank < topk, 0.0, MASK_VALUE), MASK_VALUE))
    qa_ref[0:hd, :] = q_t
    qa_ref[hd:hd + n_blocks, :] = bias.astype(BF16)
    qa_ref[hd + n_blocks:2 * hd, :] = jnp.zeros((hd - n_blocks, blk), BF16)
    q_aug = qa_ref[...]

    key_pos = lax.broadcasted_iota(jnp.int32, (blk, blk), 0)
    qry_pos = lax.broadcasted_iota(jnp.int32, (blk, blk), 1)
    s = jnp.where(key_pos <= qry_pos, _dot(kaug_ref[i], q_aug), MASK_VALUE)
    m0 = jnp.max(s, axis=0, keepdims=True)
    p = jnp.exp(s - m0)
    l0 = jnp.sum(p, axis=0, keepdims=True)
    acc0 = _dot(vt_ref[i], p.astype(BF16))

    def body(b, carry):
        m, l, acc = carry
        s = _dot(kaug_ref[b], q_aug)
        m_new = jnp.maximum(m, jnp.max(s, axis=0, keepdims=True))
        alpha = jnp.exp(m - m_new)
        p = jnp.exp(s - m_new)
        l = alpha * l + jnp.sum(p, axis=0, keepdims=True)
        acc = alpha * acc + _dot(vt_ref[b], p.astype(BF16))
        return m_new, l, acc

    _, l, acc = lax.fori_loop(0, i, body, (m0, l0, acc0))
    o_ref[...] = (acc / l).T.astype(o_ref.dtype)


def _moba_attention(h, n_heads, q_col0, k_col0, v_col0):
    t = h.shape[0]
    blk, hd = MOBA_BLOCK, HEAD_DIM
    n_blocks = t // blk
    return pl.pallas_call(
        functools.partial(_moba_kernel, n_blocks=n_blocks, topk=MOBA_TOPK),
        grid=(n_heads, n_blocks),
        in_specs=[pl.BlockSpec((blk, hd), lambda hh, i: (i, q_col0 + hh)),
                  pl.BlockSpec((t, hd), lambda hh, i: (0, k_col0 + hh)),
                  pl.BlockSpec((t, hd), lambda hh, i: (0, v_col0 + hh))],
        out_specs=pl.BlockSpec((blk, hd), lambda hh, i: (i, hh)),
        out_shape=jax.ShapeDtypeStruct((t, n_heads * hd), BF16),
        scratch_shapes=[pltpu.VMEM((n_blocks, blk, 2 * hd), BF16),
                        pltpu.VMEM((n_blocks, hd, blk), BF16),
                        pltpu.VMEM((n_blocks, hd), BF16),
                        pltpu.VMEM((n_blocks, hd), BF16),
                        pltpu.VMEM((2 * hd, blk), BF16)],
        compiler_params=_params(2),
    )(h, h, h)


def _mem_attn_kernel(q_ref, mk_ref, mv_ref, o_ref, *, scale):
    q_t = q_ref[...].astype(F32).T.astype(BF16)
    s = _dot(mk_ref[...], q_t) * scale
    m = jnp.max(s, axis=0, keepdims=True)
    p = jnp.exp(s - m)
    l = jnp.sum(p, axis=0, keepdims=True)
    v_t = mv_ref[...].astype(F32).T.astype(BF16)
    o_ref[...] = (_dot(v_t, p.astype(BF16)) / l).T.astype(o_ref.dtype)


def _mem_attention(h, q_col0, mem_kv):
    t = h.shape[0]
    mem_len = mem_kv.shape[0]
    hd = HEAD_DIM
    tq = min(MEM_TQ, t)
    return pl.pallas_call(
        functools.partial(_mem_attn_kernel, scale=HEAD_DIM ** -0.5),
        grid=(N_MEM_HEADS, t // tq),
        in_specs=[pl.BlockSpec((tq, hd), lambda hh, i: (i, q_col0 + hh)),
                  pl.BlockSpec((mem_len, hd), lambda hh, i: (0, hh)),
                  pl.BlockSpec((mem_len, hd), lambda hh, i: (0, N_MEM_HEADS + hh))],
        out_specs=pl.BlockSpec((tq, hd), lambda hh, i: (i, hh)),
        out_shape=jax.ShapeDtypeStruct((t, MEM_WIDTH), BF16),
        compiler_params=_params(2),
    )(h, mem_kv, mem_kv)


def _oproj_ln_kernel(a_ref, m_ref, wa_ref, wm_ref, x_ref, g_ref, b_ref, *rest, with_router):
    if with_router:
        wrh_ref, wrl_ref, of_ref, ob_ref, lg_ref = rest
    else:
        of_ref, ob_ref = rest
    y = _dot(a_ref[...], wa_ref[...]) + _dot(m_ref[...], wm_ref[...])
    o = _layer_norm(DEEPNORM_ALPHA * x_ref[...] + y, g_ref[...], b_ref[...])
    of_ref[...] = o
    o_hi = o.astype(BF16)
    ob_ref[...] = o_hi
    if with_router:
        o_lo = (o - o_hi.astype(F32)).astype(BF16)
        lg_ref[...] = (_dot(o_hi, wrh_ref[...]) + _dot(o_lo, wrh_ref[...])
                       + _dot(o_hi, wrl_ref[...]))


def _oproj_ln(a, m, w_o, x_res, g, b, router=None):
    t, d = x_res.shape
    wa_rows, wm_rows = a.shape[1], m.shape[1]
    tm = min(OPROJ_TM, t)
    row = lambda i: (i, 0)
    const = lambda i: (0, 0)
    in_specs = [pl.BlockSpec((tm, wa_rows), row),
                pl.BlockSpec((tm, wm_rows), row),
                pl.BlockSpec((wa_rows, d), const),
                pl.BlockSpec((wm_rows, d), lambda i: (wa_rows // wm_rows, 0)),
                pl.BlockSpec((tm, d), row),
                pl.BlockSpec((1, d), const),
                pl.BlockSpec((1, d), const)]
    args = [a, m, w_o, w_o, x_res, g.reshape(1, d), b.reshape(1, d)]
    out_specs = [pl.BlockSpec((tm, d), row), pl.BlockSpec((tm, d), row)]
    out_shape = [jax.ShapeDtypeStruct((t, d), F32), jax.ShapeDtypeStruct((t, d), BF16)]
    if router is not None:
        wr_hi, wr_lo = router
        in_specs += [pl.BlockSpec((d, LANES), const), pl.BlockSpec((d, LANES), const)]
        args += [wr_hi, wr_lo]
        out_specs.append(pl.BlockSpec((tm, LANES), row))
        out_shape.append(jax.ShapeDtypeStruct((t, LANES), F32))
    return pl.pallas_call(
        functools.partial(_oproj_ln_kernel, with_router=router is not None),
        grid=(t // tm,),
        in_specs=in_specs, out_specs=out_specs, out_shape=out_shape,
        compiler_params=_params(1),
    )(*args)


def _ffn_ln_kernel(xb_ref, xr_ref, wg_ref, wu_ref, wd_ref, g_ref, b_ref, of_ref, ob_ref):
    f = pl.program_id(1)
    wg = wg_ref[...].astype(BF16)
    wu = wu_ref[...].astype(BF16)
    wd = wd_ref[...].astype(BF16)
    for s in range(of_ref.shape[0] // FFN_SUB):
        rows = slice(s * FFN_SUB, (s + 1) * FFN_SUB)
        x = xb_ref[rows, :]
        part = _dot((_silu(_dot(x, wg)) * _dot(x, wu)).astype(BF16), wd)

        @pl.when(f == 0)
        def _():
            of_ref[rows, :] = part

        @pl.when(f > 0)
        def _():
            of_ref[rows, :] += part

        @pl.when(f == pl.num_programs(1) - 1)
        def _():
            o = _layer_norm(DEEPNORM_ALPHA * xr_ref[rows, :] + of_ref[rows, :],
                            g_ref[...], b_ref[...])
            of_ref[rows, :] = o
            ob_ref[rows, :] = o.astype(BF16)


def _ffn_ln(xb, x_res, w_gate, w_up, w_down, g, b):
    t, d = x_res.shape
    d_ff = w_gate.shape[1]
    tm = min(FFN_TM, t)
    tf = FFN_TF
    row = lambda i, f: (i, 0)
    const = lambda i, f: (0, 0)
    once = pl.Buffered(1)
    return pl.pallas_call(
        _ffn_ln_kernel,
        grid=(t // tm, d_ff // tf),
        in_specs=[pl.BlockSpec((tm, d), row, pipeline_mode=once),
                  pl.BlockSpec((tm, d), row, pipeline_mode=once),
                  pl.BlockSpec((d, tf), lambda i, f: (0, f)),
                  pl.BlockSpec((d, tf), lambda i, f: (0, f)),
                  pl.BlockSpec((tf, d), lambda i, f: (f, 0)),
                  pl.BlockSpec((1, d), const),
                  pl.BlockSpec((1, d), const)],
        out_specs=[pl.BlockSpec((tm, d), row, pipeline_mode=once),
                   pl.BlockSpec((tm, d), row, pipeline_mode=once)],
        out_shape=[jax.ShapeDtypeStruct((t, d), F32), jax.ShapeDtypeStruct((t, d), BF16)],
        compiler_params=_params(2),
    )(xb, x_res, w_gate, w_up, w_down, g.reshape(1, d), b.reshape(1, d))


def _ssd_kernel(xbc_ref, z_ref, dt_ref, cw_ref, cb_ref, dtb_ref, alog_ref, dsk_ref, ng_ref,
                e_ref, y_ref, prev_ref, st_ref, yd_ref):
    c = pl.program_id(0)
    chunk = xbc_ref.shape[0]
    d_inner, hp, n_state = SSD_D_INNER, SSD_HEAD_DIM, SSD_D_STATE
    heads_per_group = SSD_N_HEADS // SSD_N_GROUPS
    group_w = heads_per_group * hp

    @pl.when(c == 0)
    def _():
        prev_ref[...] = jnp.zeros(prev_ref.shape, F32)
        st_ref[...] = jnp.zeros(st_ref.shape, F32)

    u = xbc_ref[...].astype(F32)
    prev = prev_ref[...]
    w = cw_ref[...]
    row_id = lax.broadcasted_iota(jnp.int32, (chunk, 1), 0)
    conv = cb_ref[...] + w[SSD_CONV - 1:SSD_CONV, :] * u
    for k in range(1, SSD_CONV):
        shifted = jnp.where(row_id < k, pltpu.roll(prev, k, 0), pltpu.roll(u, k, 0))
        conv = conv + w[SSD_CONV - 1 - k:SSD_CONV - k, :] * shifted
    prev_ref[...] = u
    xbc = _silu(conv)
    xs = xbc[:, :d_inner]
    b_mat = xbc[:, d_inner:d_inner + SSD_BC_WIDTH]
    c_mat = xbc[:, d_inner + SSD_BC_WIDTH:]

    dt_in = dt_ref[...] + dtb_ref[...]
    dt = jnp.maximum(dt_in, 0.0) + jnp.log1p(jnp.exp(-jnp.abs(dt_in)))
    adt = dt * (-jnp.exp(alog_ref[...]))
    r_id = lax.broadcasted_iota(jnp.int32, (chunk, chunk), 0)
    c_id = lax.broadcasted_iota(jnp.int32, (chunk, chunk), 1)
    lower = r_id >= c_id
    tril = jnp.where(lower, 1.0, 0.0).astype(BF16)
    a1, a2, a3 = _split3(adt)
    a_cs = _dot(tril, a1) + _dot(tril, a2) + _dot(tril, a3)
    a_cs_t = a_cs.T

    expand = e_ref[...]
    s1, s2, s3 = _split3(a_cs)
    acs_e = _dot(s1, expand) + _dot(s2, expand) + _dot(s3, expand)
    d1, d2, d3 = _split3(dt)
    dt_e = _dot(d1, expand) + _dot(d2, expand) + _dot(d3, expand)
    exp_acs = jnp.exp(acs_e)
    acs_last = acs_e[chunk - 1:chunk, :]
    decay_to_end = jnp.exp(acs_last - acs_e)
    chunk_decay = exp_acs[chunk - 1:chunk, :]

    xdt = xs * dt_e
    xdt_b = xdt.astype(BF16)
    xds_b = (xdt * decay_to_end).astype(BF16)
    c_b = c_mat.astype(BF16)
    b_t = b_mat.T.astype(BF16)
    state = st_ref[...]

    for g in range(SSD_N_GROUPS):
        c_g = c_b[:, g * n_state:(g + 1) * n_state]
        bt_g = b_t[g * n_state:(g + 1) * n_state, :]
        cb = _dot(c_g, bt_g)
        for e in range(heads_per_group):
            hh = g * heads_per_group + e
            seg = jnp.where(lower, jnp.exp(a_cs[:, hh:hh + 1] - a_cs_t[hh:hh + 1, :]), 0.0)
            mixed = (cb * seg).astype(BF16)
            yd_ref[:, hh * hp:(hh + 1) * hp] = _dot(mixed, xdt_b[:, hh * hp:(hh + 1) * hp])
        cols = slice(g * group_w, (g + 1) * group_w)
        y_off = _dot(c_g, state[:, cols].astype(BF16)) * exp_acs[:, cols]
        yd_ref[:, cols] = yd_ref[:, cols] + y_off
        st_ref[:, cols] = state[:, cols] * chunk_decay[:, cols] + _dot(bt_g, xds_b[:, cols])

    y = yd_ref[...] + dsk_ref[...] * xs
    yg = y * _silu(z_ref[...].astype(F32))
    ms = jnp.mean(yg * yg, axis=-1, keepdims=True)
    y_ref[...] = (yg * lax.rsqrt(ms + RMS_EPS) * ng_ref[...]).astype(y_ref.dtype)


def _ssd_mixer(xbc, z, dt_raw, conv_w, conv_b, dt_bias, a_log, d_skip, norm_g):
    t = xbc.shape[0]
    chunk = SSD_CHUNK
    row = lambda c: (c, 0)
    const = lambda c: (0, 0)

    def lane_pad(v):
        return jnp.pad(v.astype(F32), (0, LANES - v.shape[0])).reshape(1, LANES)

    head_of_channel = jnp.arange(SSD_D_INNER) // SSD_HEAD_DIM
    expand = (jnp.arange(LANES)[:, None] == head_of_channel[None, :]).astype(BF16)
    args = [xbc, z, dt_raw, conv_w, conv_b.reshape(1, -1), lane_pad(dt_bias), lane_pad(a_log),
            jnp.repeat(d_skip.astype(F32), SSD_HEAD_DIM).reshape(1, -1),
            norm_g.astype(F32).reshape(1, -1), expand]
    in_specs = [pl.BlockSpec((chunk, SSD_CONV_DIM), row),
                pl.BlockSpec((chunk, SSD_D_INNER), row),
                pl.BlockSpec((chunk, LANES), row),
                pl.BlockSpec((SSD_CONV, SSD_CONV_DIM), const),
                pl.BlockSpec((1, SSD_CONV_DIM), const),
                pl.BlockSpec((1, LANES), const),
                pl.BlockSpec((1, LANES), const),
                pl.BlockSpec((1, SSD_D_INNER), const),
                pl.BlockSpec((1, SSD_D_INNER), const),
                pl.BlockSpec((LANES, SSD_D_INNER), const)]
    return pl.pallas_call(
        _ssd_kernel,
        grid=(t // chunk,),
        in_specs=in_specs,
        out_specs=pl.BlockSpec((chunk, SSD_D_INNER), row),
        out_shape=jax.ShapeDtypeStruct((t, SSD_D_INNER), BF16),
        scratch_shapes=[pltpu.VMEM((chunk, SSD_CONV_DIM), F32),
                        pltpu.VMEM((SSD_D_STATE, SSD_D_INNER), F32),
                        pltpu.VMEM((chunk, SSD_D_INNER), F32)],
        compiler_params=_params(1),
    )(*args)


def _moe_kernel(slot_e_ref, n_sub_ref, x_ref, wg_ref, wu_ref, wd_ref, o_ref):
    b = pl.program_id(0)
    f = pl.program_id(1)
    wg = wg_ref[0].astype(BF16)
    wu = wu_ref[0].astype(BF16)
    wd = wd_ref[0].astype(BF16)
    n_sub = n_sub_ref[b]
    for s in range(o_ref.shape[0] // MOE_SUB):
        rows = slice(s * MOE_SUB, (s + 1) * MOE_SUB)

        @pl.when(s < n_sub)
        def _():
            x = x_ref[rows, :]
            part = _dot((_silu(_dot(x, wg)) * _dot(x, wu)).astype(BF16), wd)

            @pl.when(f == 0)
            def _():
                o_ref[rows, :] = part

            @pl.when(f > 0)
            def _():
                o_ref[rows, :] += part

        @pl.when(jnp.logical_and(s >= n_sub, f == 0))
        def _():
            o_ref[rows, :] = jnp.zeros((MOE_SUB, o_ref.shape[1]), F32)


def _moe_experts(xs, slot_e, n_sub, w_gate, w_up, w_down):
    n_rows, d = xs.shape
    d_ff = w_gate.shape[2]
    n_slots = n_rows // MOE_SLOT
    tf = MOE_TF
    grid_spec = pltpu.PrefetchScalarGridSpec(
        num_scalar_prefetch=2,
        grid=(n_slots, d_ff // tf),
        in_specs=[pl.BlockSpec((MOE_SLOT, d), lambda b, f, se, ns: (b, 0)),
                  pl.BlockSpec((1, d, tf), lambda b, f, se, ns: (se[b], 0, f)),
                  pl.BlockSpec((1, d, tf), lambda b, f, se, ns: (se[b], 0, f)),
                  pl.BlockSpec((1, tf, d), lambda b, f, se, ns: (se[b], f, 0))],
        out_specs=pl.BlockSpec((MOE_SLOT, d), lambda b, f, se, ns: (b, 0)),
    )
    return pl.pallas_call(
        _moe_kernel,
        grid_spec=grid_spec,
        out_shape=jax.ShapeDtypeStruct((n_rows, d), F32),
        compiler_params=_params(2),
    )(slot_e, n_sub, xs, w_gate, w_up, w_down)


def _route(logits):
    n_tok = logits.shape[0]
    top_val, top_idx = lax.top_k(logits, TOP_K)
    gates = jax.nn.softmax(top_val, axis=-1)
    flat_e = top_idx.reshape(n_tok * TOP_K)
    onehot = (flat_e[:, None] == jnp.arange(N_EXPERTS)[None, :]).astype(jnp.int32)
    before = jnp.cumsum(onehot, axis=0) - onehot
    rank_in_e = jnp.sum(before * onehot, axis=1)
    counts = jnp.sum(onehot, axis=0)
    slots_e = (counts + MOE_SLOT - 1) // MOE_SLOT
    slot_end = jnp.cumsum(slots_e)
    slot_start = slot_end - slots_e
    dest = slot_start[flat_e] * MOE_SLOT + rank_in_e
    n_slots = n_tok * TOP_K // MOE_SLOT + N_EXPERTS
    slot_ids = jnp.arange(n_slots)
    slot_e = jnp.minimum(jnp.searchsorted(slot_end, slot_ids, side="right"), N_EXPERTS - 1)
    rows_left = counts[slot_e] - (slot_ids - slot_start[slot_e]) * MOE_SLOT
    rows_here = jnp.where(slot_ids < slot_end[-1], jnp.clip(rows_left, 0, MOE_SLOT), 0)
    n_sub = (rows_here + MOE_SUB - 1) // MOE_SUB
    return gates, dest, slot_e.astype(jnp.int32), n_sub.astype(jnp.int32), n_slots


def _add_ln_kernel(x_ref, f_ref, g_ref, b_ref, o_ref):
    o_ref[...] = _layer_norm(DEEPNORM_ALPHA * x_ref[...] + f_ref[...], g_ref[...], b_ref[...])


def _add_ln(x_res, f, g, b):
    t, d = x_res.shape
    tm = min(LN_TM, t)
    row = lambda i: (i, 0)
    const = lambda i: (0, 0)
    return pl.pallas_call(
        _add_ln_kernel,
        grid=(t // tm,),
        in_specs=[pl.BlockSpec((tm, d), row), pl.BlockSpec((tm, d), row),
                  pl.BlockSpec((1, d), const), pl.BlockSpec((1, d), const)],
        out_specs=pl.BlockSpec((tm, d), row),
        out_shape=jax.ShapeDtypeStruct((t, d), F32),
        compiler_params=_params(1),
    )(x_res, f, g.reshape(1, d), b.reshape(1, d))


def kernel(x, mem, positions, w_mem_kv, l0_w_in, l0_w_o, l0_ln1_g, l0_ln1_b, l0_ffn_w_gate, l0_ffn_w_up, l0_ffn_w_down, l0_ln2_g, l0_ln2_b, l1_w_in, l1_conv_w, l1_conv_b, l1_dt_bias, l1_a_log, l1_d_skip, l1_ssd_norm_g, l1_w_o, l1_ln1_g, l1_ln1_b, l1_router, l1_exp_w_gate, l1_exp_w_up, l1_exp_w_down, l1_ln2_g, l1_ln2_b):
    bsz, seq, d = x.shape
    assert bsz == 1
    x0 = x.reshape(seq, d)
    x0b = x0.astype(BF16)

    mem_kv = _proj(mem.reshape(-1, d).astype(BF16), w_mem_kv, 2, 0, MEM_WIDTH, BF16)

    tables = _rotary_lane_tables(positions.reshape(seq))
    n_in_blocks = (3 * ATTN_WIDTH + MEM_WIDTH) // PROJ_TN
    h0 = _proj_rot(x0b, l0_w_in, tables, n_in_blocks, PROJ_TN,
                   n_rot_blocks=2 * ATTN_WIDTH // PROJ_TN, n_q_blocks=ATTN_WIDTH // PROJ_TN,
                   q_scale=HEAD_DIM ** -0.5)
    attn = _moba_attention(h0, N_ATTN_HEADS, 0, N_ATTN_HEADS, 2 * N_ATTN_HEADS)
    memo0 = _mem_attention(h0, 3 * N_ATTN_HEADS, mem_kv)
    x1, x1b = _oproj_ln(attn, memo0, l0_w_o.astype(BF16), x0, l0_ln1_g, l0_ln1_b)
    x2, x2b = _ffn_ln(x1b, x1, l0_ffn_w_gate, l0_ffn_w_up, l0_ffn_w_down, l0_ln2_g, l0_ln2_b)

    z_blocks = SSD_D_INNER // PROJ_TN
    z = _proj(x2b, l1_w_in, z_blocks, 0, PROJ_TN, BF16)
    xbc = _proj(x2b, l1_w_in, SSD_CONV_DIM // PROJ_TN, z_blocks, PROJ_TN, BF16)
    dt_col0 = SSD_D_INNER + SSD_CONV_DIM
    w_dt = jnp.pad(l1_w_in[:, dt_col0:dt_col0 + SSD_N_HEADS], ((0, 0), (0, LANES - SSD_N_HEADS)))
    dt_raw = _proj(x2b, w_dt, 1, 0, LANES, F32)
    mq1 = _proj(x2b, l1_w_in[:, dt_col0 + SSD_N_HEADS:], 1, 0, MEM_WIDTH, BF16)
    y_ssd = _ssd_mixer(xbc, z, dt_raw, l1_conv_w, l1_conv_b, l1_dt_bias, l1_a_log, l1_d_skip,
                       l1_ssd_norm_g)
    memo1 = _mem_attention(mq1, 0, mem_kv)
    wr = jnp.pad(l1_router.astype(F32), ((0, 0), (0, LANES - N_EXPERTS)))
    wr_hi = wr.astype(BF16)
    wr_lo = (wr - wr_hi.astype(F32)).astype(BF16)
    x3, x3b, logits = _oproj_ln(y_ssd, memo1, l1_w_o.astype(BF16), x2, l1_ln1_g, l1_ln1_b,
                                router=(wr_hi, wr_lo))

    gates, dest, slot_e, n_sub, n_slots = _route(logits[:, :N_EXPERTS])
    flat_tok = jnp.repeat(jnp.arange(seq, dtype=jnp.int32), TOP_K)
    row_tok = jnp.zeros((n_slots * MOE_SLOT,), jnp.int32).at[dest].set(flat_tok)
    xs = x3b[row_tok]
    y_rows = _moe_experts(xs, slot_e, n_sub, l1_exp_w_gate, l1_exp_w_up, l1_exp_w_down)
    picked = y_rows[dest].reshape(seq, TOP_K, d)
    f = jnp.sum(picked * gates[:, :, None], axis=1)
    out = _add_ln(x3, f, l1_ln2_g, l1_ln2_b)
    return out.reshape(bsz, seq, d)
```
